```python
import math
import jax, jax.numpy as jnp
from jax import lax
import numpy as np

D_MODEL = 1024
BATCH = 16
SEQ = 2048
DEPTH = 4
DEC_BATCH = 8
DEC_SEQ = 16
PAST_LEN = 1024

CHUNK = 64
Q_BLOCK = 128
EPS = 1e-6
D_CONV = 512
CONV_K = 31
N_HEADS = 8
Q_LORA = 384
KV_LORA = 256
QK_NOPE = 64
QK_ROPE = 32
V_DIM = 64
ROPE_BASE = 10000.0
ATTN_SCALE = (QK_NOPE + QK_ROPE) ** -0.5
D_INNER = 1024
SSM_HEAD_DIM = 64
SSM_HEADS = D_INNER // SSM_HEAD_DIM
SSM_GROUPS = 4
D_STATE = 128
SSM_CONV_K = 4
XBC_DIM = D_INNER + 2 * SSM_GROUPS * D_STATE
SSD_CHUNK = CHUNK
N_BRANCH = 3
ATTN_WIDTH = N_HEADS * V_DIM
MIX_WIDTH = D_CONV + ATTN_WIDTH + D_INNER
D_FF = -(-8 * D_MODEL // (3 * 256)) * 256
IN_SIZES = (2 * D_CONV, Q_LORA, KV_LORA, QK_ROPE, D_INNER, XBC_DIM, SSM_HEADS, N_BRANCH * D_MODEL)
IN_WIDTH = sum(IN_SIZES)
IN_SPLITS = tuple(sum(IN_SIZES[:i + 1]) for i in range(len(IN_SIZES) - 1))

kernel_name = 'hybrid_streaming_encoder_step'


def rmsnorm(x, g):
    x32 = x.astype(jnp.float32)
    y = x32 * lax.rsqrt(jnp.mean(x32 * x32, axis=-1, keepdims=True) + EPS)
    return (y * g.astype(jnp.float32)).astype(x.dtype)


def layernorm(x, g, b):
    x32 = x.astype(jnp.float32)
    xc = x32 - jnp.mean(x32, axis=-1, keepdims=True)
    y = xc * lax.rsqrt(jnp.mean(xc * xc, axis=-1, keepdims=True) + EPS)
    return (y * g.astype(jnp.float32) + b.astype(jnp.float32)).astype(x.dtype)


def causal_dwconv(x_pad, w, b):
    y = lax.conv_general_dilated(x_pad, w[:, None, :].astype(x_pad.dtype), window_strides=(1,),
                                 padding='VALID', dimension_numbers=('NWC', 'WIO', 'NWC'),
                                 feature_group_count=x_pad.shape[-1])
    return y + b.astype(y.dtype)


def rope(x, pos):
    half = x.shape[-1] // 2
    inv_freq = ROPE_BASE ** (-jnp.arange(half, dtype=jnp.float32) / half)
    ang = pos.astype(jnp.float32)[:, None] * inv_freq[None, :]
    cos = jnp.cos(ang)[None, :, None, :]
    sin = jnp.sin(ang)[None, :, None, :]
    x32 = x.astype(jnp.float32)
    x1, x2 = x32[..., :half], x32[..., half:]
    return jnp.concatenate([x1 * cos - x2 * sin, x2 * cos + x1 * sin], axis=-1).astype(x.dtype)


def mla_attend(q_nope, q_rope, k_nope, k_rope, v, q_pos, k_pos):
    s = jnp.einsum('bqhd,bkhd->bhqk', q_nope, k_nope) + jnp.einsum('bqhr,bkr->bhqk', q_rope, k_rope)
    s = s.astype(jnp.float32) * ATTN_SCALE
    visible = (k_pos[None, :] // CHUNK) <= (q_pos[:, None] // CHUNK)
    p = jax.nn.softmax(jnp.where(visible, s, -jnp.inf), axis=-1).astype(v.dtype)
    return jnp.einsum('bhqk,bkhd->bqhd', p, v)


def segsum(a):
    n = a.shape[-1]
    idx = jnp.arange(n)
    cs = jnp.cumsum(jnp.where(idx[:, None] > idx[None, :], a[..., :, None], 0.0), axis=-2)
    return jnp.where(idx[:, None] >= idx[None, :], cs, -jnp.inf)


def ssd(x, dt, a, bm, cm, h0, chunk):
    bsz, T, H, P = x.shape
    nc = T // chunk
    f32 = jnp.float32
    xdt = (x.astype(f32) * dt[..., None]).reshape(bsz, nc, chunk, H, P)
    bm = bm.astype(f32).reshape(bsz, nc, chunk, H, D_STATE)
    cm = cm.astype(f32).reshape(bsz, nc, chunk, H, D_STATE)
    da = jnp.moveaxis((dt * a).reshape(bsz, nc, chunk, H), 3, 1)
    da_cs = jnp.cumsum(da, axis=-1)
    decay_in = jnp.exp(segsum(da))
    y_diag = jnp.einsum('bclhn,bcshn,bhcls,bcshp->bclhp', cm, bm, decay_in, xdt)
    decay_to_end = jnp.exp(da_cs[..., -1:] - da_cs)
    states = jnp.einsum('bclhn,bhcl,bclhp->bchpn', bm, decay_to_end, xdt)
    states = jnp.concatenate([h0.astype(f32)[:, None], states], axis=1)
    decay_chunk = jnp.exp(segsum(jnp.pad(da_cs[..., -1], ((0, 0), (0, 0), (1, 0)))))
    states = jnp.einsum('bhzc,bchpn->bzhpn', decay_chunk, states)
    y_off = jnp.einsum('bclhn,bchpn,bhcl->bclhp', cm, states[:, :-1], jnp.exp(da_cs))
    y = (y_diag + y_off).reshape(bsz, T, H, P)
    return y.astype(x.dtype), states[:, -1].astype(h0.dtype)


def trunk_layer(x, past_lat, past_kr, conv_buf, ssm_buf, ssm_h0,
                g_pre_mix, g_post_mix, g_pre_ffn, g_post_ffn, w_in,
                conv_w, conv_b, conv_ln_g, conv_ln_b, g_q, w_uq, g_kv, w_ukv,
                ssm_conv_w, ssm_conv_b, dt_bias, a_log, d_skip, g_ssm,
                w_mix_out, w_out, w_gate_up, w_down):
    bsz, T, _ = x.shape
    n_past = past_lat.shape[1]
    pos = n_past + jnp.arange(T)
    h = rmsnorm(x, g_pre_mix)
    u = h @ w_in
    u_glu, u_q, u_kv, u_kr, u_z, u_xbc, u_dt, u_gate = jnp.split(u, IN_SPLITS, axis=-1)

    a = u_glu[..., :D_CONV] * jax.nn.sigmoid(u_glu[..., D_CONV:])
    a_pad = jnp.concatenate([conv_buf, a], axis=1)
    c = jax.nn.silu(layernorm(causal_dwconv(a_pad, conv_w, conv_b), conv_ln_g, conv_ln_b))
    o_conv = c @ w_mix_out[:D_CONV]

    q = (rmsnorm(u_q, g_q) @ w_uq).reshape(bsz, T, N_HEADS, QK_NOPE + QK_ROPE)
    q_nope, q_rope = q[..., :QK_NOPE], rope(q[..., QK_NOPE:], pos)
    lat = rmsnorm(u_kv, g_kv)
    kr = rope(u_kr[:, :, None, :], pos)[:, :, 0, :]
    lat_all = jnp.concatenate([past_lat, lat], axis=1)
    kr_all = jnp.concatenate([past_kr, kr], axis=1)
    k_pos = jnp.arange(n_past + T)
    kv = (lat_all @ w_ukv).reshape(bsz, n_past + T, N_HEADS, QK_NOPE + V_DIM)
    k_nope, v = kv[..., :QK_NOPE], kv[..., QK_NOPE:]
    blk = min(Q_BLOCK, T)
    nb = T // blk

    def to_blocks(t):
        return jnp.moveaxis(t.reshape(bsz, nb, blk, *t.shape[2:]), 1, 0)

    o = lax.map(lambda qs: mla_attend(qs[0], qs[1], k_nope, kr_all, v, qs[2], k_pos),
                (to_blocks(q_nope), to_blocks(q_rope), pos.reshape(nb, blk)))
    o = jnp.moveaxis(o, 0, 1).reshape(bsz, T, ATTN_WIDTH)
    o_attn = o @ w_mix_out[D_CONV:D_CONV + ATTN_WIDTH]

    xbc_pad = jnp.concatenate([ssm_buf, u_xbc], axis=1)
    xbc = jax.nn.silu(causal_dwconv(xbc_pad, ssm_conv_w, ssm_conv_b))
    xs, bm, cm = jnp.split(xbc, (D_INNER, D_INNER + SSM_GROUPS * D_STATE), axis=-1)
    xs = xs.reshape(bsz, T, SSM_HEADS, SSM_HEAD_DIM)
    rep = SSM_HEADS // SSM_GROUPS
    bm = jnp.repeat(bm.reshape(bsz, T, SSM_GROUPS, D_STATE), rep, axis=2)
    cm = jnp.repeat(cm.reshape(bsz, T, SSM_GROUPS, D_STATE), rep, axis=2)
    dt = jax.nn.softplus((u_dt + dt_bias).astype(jnp.float32))
    a_neg = -jnp.exp(a_log.astype(jnp.float32))
    y, h_new = ssd(xs, dt, a_neg, bm, cm, ssm_h0, min(SSD_CHUNK, T))
    y = (y + xs * d_skip[:, None]).reshape(bsz, T, D_INNER) * jax.nn.silu(u_z)
    y = rmsnorm(y.reshape(bsz, T, SSM_GROUPS, D_INNER // SSM_GROUPS),
                g_ssm.reshape(SSM_GROUPS, D_INNER // SSM_GROUPS)).reshape(bsz, T, D_INNER)
    o_ssm = y @ w_mix_out[D_CONV + ATTN_WIDTH:]

    gates = jax.nn.sigmoid(u_gate).reshape(bsz, T, N_BRANCH, D_MODEL)
    merged = gates[..., 0, :] * o_conv + gates[..., 1, :] * o_attn + gates[..., 2, :] * o_ssm
    x = x + rmsnorm(merged @ w_out, g_post_mix)
    f = rmsnorm(x, g_pre_ffn) @ w_gate_up
    f = (jax.nn.silu(f[..., :D_FF]) * f[..., D_FF:]) @ w_down
    x = x + rmsnorm(f, g_post_ffn)
    return x, lat, kr, a_pad[:, -(CONV_K - 1):], xbc_pad[:, -(SSM_CONV_K - 1):], h_new


def setup_inputs(seed: int = 0) -> dict:
    key = jax.random.key(seed)
    k = jax.random.split(key, 32)
    f32 = jnp.float32

    def nrm(kk, shape, scale):
        return scale * jax.random.normal(kk, shape, f32)

    def gain(kk, shape):
        return 1.0 + 0.05 * jax.random.normal(kk, shape, f32)

    dt0 = jnp.exp(jax.random.uniform(k[20], (DEPTH, SSM_HEADS), f32, math.log(1e-3), math.log(1e-1)))
    return {
        'x_prompt': nrm(k[0], (BATCH, SEQ, D_MODEL), 1.0),
        'x_sample': nrm(k[1], (DEC_BATCH, DEC_SEQ, D_MODEL), 1.0),
        'cache_mla_latent': nrm(k[2], (DEPTH, DEC_BATCH, PAST_LEN, KV_LORA), 1.0),
        'cache_mla_rope': nrm(k[3], (DEPTH, DEC_BATCH, PAST_LEN, QK_ROPE), 1.0),
        'state_conv': nrm(k[4], (DEPTH, DEC_BATCH, CONV_K - 1, D_CONV), 0.5),
        'state_ssm_conv': nrm(k[5], (DEPTH, DEC_BATCH, SSM_CONV_K - 1, XBC_DIM), 1.0),
        'state_ssm': nrm(k[6], (DEPTH, DEC_BATCH, SSM_HEADS, SSM_HEAD_DIM, D_STATE), 0.1),
        'g_pre_mix': gain(k[7], (DEPTH, D_MODEL)),
        'g_post_mix': gain(k[8], (DEPTH, D_MODEL)),
        'g_pre_ffn': gain(k[9], (DEPTH, D_MODEL)),
        'g_post_ffn': gain(k[10], (DEPTH, D_MODEL)),
        'w_in': nrm(k[11], (DEPTH, D_MODEL, IN_WIDTH), D_MODEL ** -0.5),
        'conv_w': nrm(k[12], (DEPTH, CONV_K, D_CONV), CONV_K ** -0.5),
        'conv_b': nrm(k[13], (DEPTH, D_CONV), 0.02),
        'conv_ln_g': gain(k[14], (DEPTH, D_CONV)),
        'conv_ln_b': nrm(k[15], (DEPTH, D_CONV), 0.02),
        'g_q': gain(k[16], (DEPTH, Q_LORA)),
        'w_uq': nrm(k[17], (DEPTH, Q_LORA, N_HEADS * (QK_NOPE + QK_ROPE)), Q_LORA ** -0.5),
        'g_kv': gain(k[18], (DEPTH, KV_LORA)),
        'w_ukv': nrm(k[19], (DEPTH, KV_LORA, N_HEADS * (QK_NOPE + V_DIM)), KV_LORA ** -0.5),
        'ssm_conv_w': nrm(k[21], (DEPTH, SSM_CONV_K, XBC_DIM), SSM_CONV_K ** -0.5),
        'ssm_conv_b': nrm(k[22], (DEPTH, XBC_DIM), 0.02),
        'dt_bias': dt0 + jnp.log(-jnp.expm1(-dt0)),
        'a_log': jnp.log(jax.random.uniform(k[23], (DEPTH, SSM_HEADS), f32, 1.0, 16.0)),
        'd_skip': gain(k[24], (DEPTH, SSM_HEADS)),
        'g_ssm': gain(k[25], (DEPTH, D_INNER)),
        'w_mix_out': nrm(k[26], (DEPTH, MIX_WIDTH, D_MODEL), D_MODEL ** -0.5),
        'w_out': nrm(k[27], (DEPTH, D_MODEL, D_MODEL), D_MODEL ** -0.5),
        'w_gate_up': nrm(k[28], (DEPTH, D_MODEL, 2 * D_FF), D_MODEL ** -0.5),
        'w_down': nrm(k[29], (DEPTH, D_FF, D_MODEL), D_FF ** -0.5),
    }


def reference(x_prompt, x_sample, cache_mla_latent, cache_mla_rope, state_conv, state_ssm_conv, state_ssm,
              g_pre_mix, g_post_mix, g_pre_ffn, g_post_ffn, w_in, conv_w, conv_b, conv_ln_g, conv_ln_b,
              g_q, w_uq, g_kv, w_ukv, ssm_conv_w, ssm_conv_b, dt_bias, a_log, d_skip, g_ssm,
              w_mix_out, w_out, w_gate_up, w_down):
    bp, dtype = x_prompt.shape[0], x_prompt.dtype
    no_lat = jnp.zeros((bp, 0, KV_LORA), dtype)
    no_kr = jnp.zeros((bp, 0, QK_ROPE), dtype)
    zero_conv = jnp.zeros((bp, CONV_K - 1, D_CONV), dtype)
    zero_ssm_conv = jnp.zeros((bp, SSM_CONV_K - 1, XBC_DIM), dtype)
    zero_ssm = jnp.zeros((bp, SSM_HEADS, SSM_HEAD_DIM, D_STATE), dtype)
    yp, ys = x_prompt, x_sample
    p_lat, p_kr, p_conv, p_sconv, p_ssm = [], [], [], [], []
    s_lat, s_kr, s_conv, s_sconv, s_ssm = [], [], [], [], []
    for l in range(DEPTH):
        lw = (g_pre_mix[l], g_post_mix[l], g_pre_ffn[l], g_post_ffn[l], w_in[l],
              conv_w[l], conv_b[l], conv_ln_g[l], conv_ln_b[l], g_q[l], w_uq[l], g_kv[l], w_ukv[l],
              ssm_conv_w[l], ssm_conv_b[l], dt_bias[l], a_log[l], d_skip[l], g_ssm[l],
              w_mix_out[l], w_out[l], w_gate_up[l], w_down[l])
        yp, lat, kr, cb, sb, hs = trunk_layer(yp, no_lat, no_kr, zero_conv, zero_ssm_conv, zero_ssm, *lw)
        p_lat.append(lat)
        p_kr.append(kr)
        p_conv.append(cb)
        p_sconv.append(sb)
        p_ssm.append(hs)
        ys, lat, kr, cb, sb, hs = trunk_layer(ys, cache_mla_latent[l], cache_mla_rope[l], state_conv[l],
                                              state_ssm_conv[l], state_ssm[l], *lw)
        s_lat.append(lat)
        s_kr.append(kr)
        s_conv.append(cb)
        s_sconv.append(sb)
        s_ssm.append(hs)
    return (yp, ys, jnp.stack(p_lat), jnp.stack(p_kr), jnp.stack(p_conv), jnp.stack(p_sconv), jnp.stack(p_ssm),
            jnp.stack(s_lat), jnp.stack(s_kr), jnp.stack(s_conv), jnp.stack(s_sconv), jnp.stack(s_ssm))
```

```python
import functools

import jax
import jax.numpy as jnp
from jax import lax
from jax.experimental import pallas as pl
from jax.experimental.pallas import tpu as pltpu

F32 = jnp.float32
BF16 = jnp.bfloat16

CHUNK = 64
EPS = 1e-6
ROPE_BASE = 10000.0
N_HEADS = 8
QK_NOPE = 64
QK_ROPE = 32
V_DIM = 64
SSM_HEAD_DIM = 64
SSM_GROUPS = 4
D_STATE = 128
CONV_K = 31
SSM_CONV_K = 4

LANES = 128
SUBLANES = 8
HEAD_SLAB = LANES
CONV_HALO = 32
SSM_HALO = SUBLANES
VMEM_LIMIT_BYTES = 56 * 1024 * 1024


def _cparams(n_grid):
    return pltpu.CompilerParams(dimension_semantics=("arbitrary",) * n_grid,
                                vmem_limit_bytes=VMEM_LIMIT_BYTES)


def _row_tile(rows, cap):
    best = None
    for t in range(16, min(rows, cap) + 1, 16):
        if rows % t == 0:
            best = t
    assert best is not None, rows
    return best


def _const_spec(shape, layer=None):
    nd = len(shape)
    if layer is None:
        return pl.BlockSpec(shape, lambda *_: (0,) * nd, pipeline_mode=pl.Buffered(1))
    return pl.BlockSpec((None,) + tuple(shape), lambda *_: (layer,) + (0,) * nd,
                        pipeline_mode=pl.Buffered(1))


def _rms(x, g):
    return x * lax.rsqrt(jnp.mean(x * x, axis=-1, keepdims=True) + EPS) * g


def _silu(x):
    return x * jax.nn.sigmoid(x)


def _dot(a, b):
    return jnp.dot(a, b, preferred_element_type=F32)


def _dot_nt(a, b):
    return lax.dot_general(a, b, (((1,), (1,)), ((), ())), preferred_element_type=F32)


def _dot_tn(a, b):
    return lax.dot_general(a, b, (((0,), (0,)), ((), ())), preferred_element_type=F32)


def _split3(x):
    hi = x.astype(BF16)
    r = x - hi.astype(F32)
    mid = r.astype(BF16)
    lo = (r - mid.astype(F32)).astype(BF16)
    return hi, mid, lo


def _inproj_kernel(x_ref, g_ref, w_ref, gq_ref, wuq_ref, gkv_ref, dtb_ref, cq_ref, sq_ref, ck_ref, sk_ref,
                   a_ref, q_ref, lat_ref, kr_ref, z_ref, xbc_ref, dt_ref, gate_ref, *, dims):
    d_conv, q_lora, kv_lora, d_inner, xbc_dim, n_ssm_heads, d_model = dims
    h = _rms(x_ref[...], g_ref[...]).astype(BF16)
    col = [0]

    def proj(width):
        c0 = col[0]
        col[0] = c0 + width
        return _dot(h, w_ref[:, c0:c0 + width])

    u_val = proj(d_conv)
    u_gate = proj(d_conv)
    a_ref[...] = u_val * jax.nn.sigmoid(u_gate)

    hq = _rms(proj(q_lora), gq_ref[...]).astype(BF16)
    qs = _dot(hq, wuq_ref[...])
    cq = cq_ref[...]
    sq = sq_ref[...]
    for hd in range(N_HEADS):
        slab = qs[:, hd * HEAD_SLAB:(hd + 1) * HEAD_SLAB]
        rot = pltpu.roll(slab, HEAD_SLAB - QK_ROPE, axis=1)
        q_ref[:, hd * HEAD_SLAB:(hd + 1) * HEAD_SLAB] = (slab * cq + rot * sq).astype(BF16)

    lat_ref[...] = _rms(proj(kv_lora), gkv_ref[...])

    z_ref[...] = _silu(proj(d_inner)).astype(BF16)
    half = xbc_dim // 2
    xbc_ref[:, :half] = proj(half).astype(BF16)
    xbc_ref[:, half:] = proj(half).astype(BF16)

    for i in range(3):
        gate_ref[:, i * d_model:(i + 1) * d_model] = jax.nn.sigmoid(proj(d_model)).astype(BF16)

    m = proj(LANES)
    kr_ref[...] = m[:, :QK_ROPE] * ck_ref[...] + m[:, QK_ROPE:2 * QK_ROPE] * sk_ref[...]
    dt_raw = m[:, 2 * QK_ROPE:2 * QK_ROPE + n_ssm_heads] + dtb_ref[...]
    dt_ref[...] = jnp.maximum(dt_raw, 0.0) + jnp.log1p(jnp.exp(-jnp.abs(dt_raw)))


def _inproj(xf, layer, wts, tabs, dims, tm):
    m_rows, d_model = xf.shape
    d_conv, q_lora, kv_lora, d_inner, xbc_dim, n_ssm_heads, _ = dims
    n_in = wts["w1"].shape[-1]
    n_tab = tabs["cq"].shape[0] // tm
    row = lambda w: pl.BlockSpec((tm, w), lambda i: (i, 0))
    tab = lambda w: pl.BlockSpec((tm, w), lambda i: (i % n_tab, 0))
    out_shapes = (
        jax.ShapeDtypeStruct((m_rows, d_conv), F32),
        jax.ShapeDtypeStruct((m_rows, N_HEADS * HEAD_SLAB), BF16),
        jax.ShapeDtypeStruct((m_rows, kv_lora), F32),
        jax.ShapeDtypeStruct((m_rows, QK_ROPE), F32),
        jax.ShapeDtypeStruct((m_rows, d_inner), BF16),
        jax.ShapeDtypeStruct((m_rows, xbc_dim), BF16),
        jax.ShapeDtypeStruct((m_rows, n_ssm_heads), F32),
        jax.ShapeDtypeStruct((m_rows, 3 * d_model), BF16),
    )
    return pl.pallas_call(
        functools.partial(_inproj_kernel, dims=dims),
        grid=(m_rows // tm,),
        in_specs=[
            row(d_model),
            _const_spec((1, d_model), layer),
            _const_spec((d_model, n_in), layer),
            _const_spec((1, q_lora), layer),
            _const_spec((q_lora, N_HEADS * HEAD_SLAB), layer),
            _const_spec((1, kv_lora), layer),
            _const_spec((1, n_ssm_heads), layer),
            tab(HEAD_SLAB), tab(HEAD_SLAB), tab(QK_ROPE), tab(QK_ROPE),
        ],
        out_specs=tuple(row(s.shape[1]) for s in out_shapes),
        out_shape=out_shapes,
        compiler_params=_cparams(1),
        name="inproj",
    )(xf, wts["g_pre_mix"], wts["w1"], wts["g_q"], wts["wuq"], wts["g_kv"], wts["dt_bias"],
      tabs["cq"], tabs["sq"], tabs["ck"], tabs["sk"])


def _kvproj_kernel(lat_ref, kr_ref, wk_ref, pk_ref, wv_ref, k_ref, v_ref):
    lat = lat_ref[...].astype(BF16)
    kr = kr_ref[...].astype(BF16)
    k_ref[...] = (_dot(lat, wk_ref[...]) + _dot(kr, pk_ref[...])).astype(BF16)
    v_ref[...] = _dot(lat, wv_ref[...]).astype(BF16)


def _kvproj(lat, kr, layer, wts, rows, row_offset=0):
    kv_lora = lat.shape[1]
    tr = _row_tile(rows, 1024)
    assert row_offset % tr == 0
    off = row_offset // tr
    k_w = N_HEADS * HEAD_SLAB
    v_w = N_HEADS * V_DIM
    return pl.pallas_call(
        _kvproj_kernel,
        grid=(rows // tr,),
        in_specs=[
            pl.BlockSpec((tr, kv_lora), lambda i: (off + i, 0)),
            pl.BlockSpec((tr, QK_ROPE), lambda i: (off + i, 0)),
            _const_spec((kv_lora, k_w), layer),
            _const_spec((QK_ROPE, k_w)),
            _const_spec((kv_lora, v_w), layer),
        ],
        out_specs=(pl.BlockSpec((tr, k_w), lambda i: (i, 0)), pl.BlockSpec((tr, v_w), lambda i: (i, 0))),
        out_shape=(jax.ShapeDtypeStruct((rows, k_w), BF16), jax.ShapeDtypeStruct((rows, v_w), BF16)),
        compiler_params=_cparams(1),
        name="kvproj",
    )(lat, kr, wts["wk"], wts["pk"], wts["wv"])


def _attn_kernel(q_ref, k_ref, v_ref, o_ref, *, tq):
    i = pl.program_id(1)
    lane = lax.broadcasted_iota(jnp.int32, (tq, LANES), 1)
    r_chunk = lax.broadcasted_iota(jnp.int32, (tq, tq), 0) // CHUNK
    c_chunk = lax.broadcasted_iota(jnp.int32, (tq, tq), 1) // CHUNK
    diag_visible = c_chunk <= r_chunk
    pair = LANES // V_DIM
    for hp in range(N_HEADS // pair):
        outs = []
        for jj in range(pair):
            hd = pair * hp + jj
            qh = q_ref[0, :, hd * HEAD_SLAB:(hd + 1) * HEAD_SLAB]

            def step(kb, carry, masked, qh=qh, hd=hd, hp=hp):
                m, l, acc = carry
                start = pl.multiple_of(kb * tq, tq)
                kblk = k_ref[0, pl.ds(start, tq), hd * HEAD_SLAB:(hd + 1) * HEAD_SLAB]
                s = _dot_nt(qh, kblk)
                if masked:
                    s = jnp.where(diag_visible, s, -jnp.inf)
                m_new = jnp.maximum(m, jnp.max(s, axis=-1, keepdims=True))
                alpha = jnp.exp(m - m_new)
                p = jnp.exp(s - m_new)
                l = alpha * l + jnp.sum(p, axis=-1, keepdims=True)
                vblk = v_ref[0, pl.ds(start, tq), hp * LANES:(hp + 1) * LANES]
                acc = alpha * acc + _dot(p.astype(BF16), vblk)
                return m_new, l, acc

            init = (jnp.full((tq, 1), -jnp.inf, F32), jnp.zeros((tq, 1), F32), jnp.zeros((tq, LANES), F32))
            carry = lax.fori_loop(0, i, functools.partial(step, masked=False), init)
            _, l, acc = step(i, carry, True)
            outs.append(acc / l)
        o_ref[0, :, hp * LANES:(hp + 1) * LANES] = jnp.where(lane < V_DIM, outs[0], outs[1]).astype(BF16)


def _attn_prompt(q, k, v, tq):
    b, t, qw = q.shape
    vw = v.shape[-1]
    return pl.pallas_call(
        functools.partial(_attn_kernel, tq=tq),
        grid=(b, t // tq),
        in_specs=[
            pl.BlockSpec((1, tq, qw), lambda bi, i: (bi, i, 0)),
            pl.BlockSpec((1, t, qw), lambda bi, i: (bi, 0, 0)),
            pl.BlockSpec((1, t, vw), lambda bi, i: (bi, 0, 0)),
        ],
        out_specs=pl.BlockSpec((1, tq, vw), lambda bi, i: (bi, i, 0)),
        out_shape=jax.ShapeDtypeStruct((b, t, vw), BF16),
        compiler_params=_cparams(2),
        name="attn_prompt",
    )(q, k, v)


def _attn_hist_kernel(q_ref, kp_ref, vp_ref, kn_ref, vn_ref, o_ref, *, n_past):
    tq = q_ref.shape[1]
    tp = kp_ref.shape[1]
    lane = lax.broadcasted_iota(jnp.int32, (tq, LANES), 1)
    q_chunk_p = (n_past + lax.broadcasted_iota(jnp.int32, (tq, tp), 0)) // CHUNK
    vis_p = (lax.broadcasted_iota(jnp.int32, (tq, tp), 1) // CHUNK) <= q_chunk_p
    q_chunk_n = (n_past + lax.broadcasted_iota(jnp.int32, (tq, tq), 0)) // CHUNK
    vis_n = ((n_past + lax.broadcasted_iota(jnp.int32, (tq, tq), 1)) // CHUNK) <= q_chunk_n
    pair = LANES // V_DIM
    for hp in range(N_HEADS // pair):
        outs = []
        for jj in range(pair):
            hd = pair * hp + jj
            hs = slice(hd * HEAD_SLAB, (hd + 1) * HEAD_SLAB)
            qh = q_ref[0, :, hs]
            s_p = jnp.where(vis_p, _dot_nt(qh, kp_ref[0, :, hs]), -jnp.inf)
            s_n = jnp.where(vis_n, _dot_nt(qh, kn_ref[0, :, hs]), -jnp.inf)
            m = jnp.maximum(jnp.max(s_p, axis=-1, keepdims=True), jnp.max(s_n, axis=-1, keepdims=True))
            p_p = jnp.exp(s_p - m)
            p_n = jnp.exp(s_n - m)
            l = jnp.sum(p_p, axis=-1, keepdims=True) + jnp.sum(p_n, axis=-1, keepdims=True)
            vs = slice(hp * LANES, (hp + 1) * LANES)
            acc = _dot(p_p.astype(BF16), vp_ref[0, :, vs]) + _dot(p_n.astype(BF16), vn_ref[0, :, vs])
            outs.append(acc / l)
        o_ref[0, :, hp * LANES:(hp + 1) * LANES] = jnp.where(lane < V_DIM, outs[0], outs[1]).astype(BF16)


def _attn_hist(q, kp, vp, kn, vn, n_past):
    b, t, qw = q.shape
    tp = kp.shape[1]
    vw = vp.shape[-1]
    blk = lambda rows, w: pl.BlockSpec((1, rows, w), lambda bi: (bi, 0, 0))
    return pl.pallas_call(
        functools.partial(_attn_hist_kernel, n_past=n_past),
        grid=(b,),
        in_specs=[blk(t, qw), blk(tp, qw), blk(tp, vw), blk(t, qw), blk(t, vw)],
        out_specs=blk(t, vw),
        out_shape=jax.ShapeDtypeStruct((b, t, vw), BF16),
        compiler_params=_cparams(1),
        name="attn_hist",
    )(q, kp, vp, kn, vn)


def _conv_kernel(a_ref, st_ref, w_ref, b_ref, lg_ref, lb_ref, c_ref, pad_ref, y_ref):
    j = pl.program_id(1)
    tc = a_ref.shape[1]
    d_conv = a_ref.shape[2]

    @pl.when(j == 0)
    def _():
        pad_ref[0:CONV_HALO, :] = st_ref[0]

    pad_ref[CONV_HALO:CONV_HALO + tc, :] = a_ref[0]
    first = CONV_HALO - (CONV_K - 1)
    rc = min(tc, 128)
    for c0 in range(0, d_conv, LANES):
        for r0 in range(0, tc, rc):
            acc = jnp.zeros((rc, LANES), F32) + b_ref[:, c0:c0 + LANES]
            for k in range(CONV_K):
                lo = r0 + first + k
                acc = acc + pad_ref[lo:lo + rc, c0:c0 + LANES] * w_ref[k:k + 1, c0:c0 + LANES]
            y_ref[r0:r0 + rc, c0:c0 + LANES] = acc
    rl = min(tc, 64)
    for r0 in range(0, tc, rl):
        y = y_ref[r0:r0 + rl, :]
        yc = y - jnp.mean(y, axis=-1, keepdims=True)
        yn = yc * lax.rsqrt(jnp.mean(yc * yc, axis=-1, keepdims=True) + EPS) * lg_ref[...] + lb_ref[...]
        c_ref[0, r0:r0 + rl, :] = _silu(yn).astype(BF16)
    pad_ref[0:CONV_HALO, :] = pad_ref[tc:tc + CONV_HALO, :]


def _conv(a, state_pad, layer, wts, tc):
    b, t, d_conv = a.shape
    return pl.pallas_call(
        _conv_kernel,
        grid=(b, t // tc),
        in_specs=[
            pl.BlockSpec((1, tc, d_conv), lambda bi, j: (bi, j, 0)),
            pl.BlockSpec((1, CONV_HALO, d_conv), lambda bi, j: (bi, 0, 0)),
            _const_spec((CONV_K, d_conv), layer),
            _const_spec((1, d_conv), layer),
            _const_spec((1, d_conv), layer),
            _const_spec((1, d_conv), layer),
        ],
        out_specs=pl.BlockSpec((1, tc, d_conv), lambda bi, j: (bi, j, 0)),
        out_shape=jax.ShapeDtypeStruct((b, t, d_conv), BF16),
        scratch_shapes=[pltpu.VMEM((tc + CONV_HALO, d_conv), F32), pltpu.VMEM((tc, d_conv), F32)],
        compiler_params=_cparams(2),
        name="conv",
    )(a, state_pad, wts["conv_w"], wts["conv_b"], wts["conv_ln_g"], wts["conv_ln_b"])


def _ssd_kernel(xbc_ref, st_ref, dt_ref, z_ref, h0_ref, cw_ref, cb_ref, alog_ref, dsk_ref, gs_ref,
                y_ref, hT_ref, pad_ref, xs_ref, bm_ref, cm_ref, dtf_ref, csf_ref, yacc_ref):
    j = pl.program_id(1)
    blk = xbc_ref.shape[1]
    xbc_dim = xbc_ref.shape[2]
    d_inner = z_ref.shape[2]
    n_heads = dt_ref.shape[2]
    gw = SSM_GROUPS * D_STATE
    heads_per_group = n_heads // SSM_GROUPS
    group_w = heads_per_group * SSM_HEAD_DIM

    @pl.when(j == 0)
    def _():
        pad_ref[0:SSM_HALO, :] = st_ref[0]
        hT_ref[0] = h0_ref[0]

    pad_ref[SSM_HALO:SSM_HALO + blk, :] = xbc_ref[0].astype(F32)
    first = SSM_HALO - (SSM_CONV_K - 1)
    rc = min(blk, 64)
    cc = 2 * LANES
    for c0 in range(0, xbc_dim, cc):
        for r0 in range(0, blk, rc):
            acc = jnp.zeros((rc, cc), F32) + cb_ref[:, c0:c0 + cc]
            for k in range(SSM_CONV_K):
                lo = r0 + first + k
                acc = acc + pad_ref[lo:lo + rc, c0:c0 + cc] * cw_ref[k:k + 1, c0:c0 + cc]
            acc = _silu(acc)
            if c0 < d_inner:
                xs_ref[r0:r0 + rc, c0:c0 + cc] = acc
            elif c0 < d_inner + gw:
                bm_ref[r0:r0 + rc, c0 - d_inner:c0 - d_inner + cc] = acc.astype(BF16)
            else:
                cm_ref[r0:r0 + rc, c0 - d_inner - gw:c0 - d_inner - gw + cc] = acc.astype(BF16)
    pad_ref[0:SSM_HALO, :] = pad_ref[blk:blk + SSM_HALO, :]

    dt = dt_ref[0]
    da = dt * (-jnp.exp(alog_ref[...]))
    r_i = lax.broadcasted_iota(jnp.int32, (blk, blk), 0)
    c_i = lax.broadcasted_iota(jnp.int32, (blk, blk), 1)
    causal = c_i <= r_i
    tri = causal.astype(BF16)
    tri_u = (r_i <= c_i).astype(BF16)
    da3 = _split3(da)
    cs = sum(_dot(tri, p) for p in da3)
    cs_t = sum(_dot_tn(p, tri_u) for p in da3)
    e_h = lax.broadcasted_iota(jnp.int32, (n_heads, d_inner), 0)
    e_c = lax.broadcasted_iota(jnp.int32, (n_heads, d_inner), 1) // SSM_HEAD_DIM
    expand = (e_h == e_c).astype(BF16)
    dtf_ref[...] = sum(_dot(p, expand) for p in _split3(dt))
    csf_ref[...] = sum(_dot(p, expand) for p in _split3(cs))
    et_r = lax.broadcasted_iota(jnp.int32, (d_inner, n_heads), 0) // SSM_HEAD_DIM
    et_c = lax.broadcasted_iota(jnp.int32, (d_inner, n_heads), 1)
    expand_t = (et_r == et_c).astype(BF16)
    cs_last_col = jnp.broadcast_to(cs_t[:, blk - 1:blk], (n_heads, D_STATE))
    h_decay = jnp.exp(sum(_dot(expand_t, p) for p in _split3(cs_last_col)))

    lane = lax.broadcasted_iota(jnp.int32, (blk, LANES), 1)
    heads_per_slab = LANES // SSM_HEAD_DIM
    for g in range(SSM_GROUPS):
        bg = bm_ref[:, g * D_STATE:(g + 1) * D_STATE]
        cg = cm_ref[:, g * D_STATE:(g + 1) * D_STATE]
        gs = slice(g * group_w, (g + 1) * group_w)
        csf_g = csf_ref[:, gs]
        xdt_g = xs_ref[:, gs] * dtf_ref[:, gs]
        cb_mat = _dot_nt(cg, bg)
        for sl in range(group_w // LANES):
            lo = g * group_w + sl * LANES
            xdt_slab = xdt_g[:, sl * LANES:(sl + 1) * LANES].astype(BF16)
            parts = []
            for hh in range(heads_per_slab):
                hd = lo // SSM_HEAD_DIM + hh
                seg = cs[:, hd:hd + 1] - cs_t[hd:hd + 1, :]
                dec = jnp.exp(jnp.where(causal, seg, -jnp.inf))
                parts.append(_dot((cb_mat * dec).astype(BF16), xdt_slab))
            yacc_ref[:, lo:lo + LANES] = jnp.where(lane < SSM_HEAD_DIM, parts[0], parts[1])
        h_old = hT_ref[0, gs, :]
        y_off = _dot_nt(cg, h_old.astype(BF16)) * jnp.exp(csf_g)
        yacc_ref[:, gs] = yacc_ref[:, gs] + y_off
        to_end = jnp.exp(csf_g[blk - 1:blk, :] - csf_g)
        h_in = _dot_tn((xdt_g * to_end).astype(BF16), bg)
        hT_ref[0, gs, :] = h_old * h_decay[gs, :] + h_in

    norm_w = d_inner // SSM_GROUPS
    for g in range(SSM_GROUPS):
        gs = slice(g * norm_w, (g + 1) * norm_w)
        y = (yacc_ref[:, gs] + xs_ref[:, gs] * dsk_ref[:, gs]) * z_ref[0, :, gs].astype(F32)
        y_ref[0, :, gs] = _rms(y, gs_ref[:, gs]).astype(BF16)


def _ssd(xbc, state_pad, dt, z, h0, layer, wts, blk):
    b, t, xbc_dim = xbc.shape
    d_inner = z.shape[-1]
    n_heads = dt.shape[-1]
    gw = SSM_GROUPS * D_STATE
    seq = lambda w: pl.BlockSpec((1, blk, w), lambda bi, j: (bi, j, 0))
    per_b = lambda r, w: pl.BlockSpec((1, r, w), lambda bi, j: (bi, 0, 0))
    return pl.pallas_call(
        _ssd_kernel,
        grid=(b, t // blk),
        in_specs=[
            seq(xbc_dim), per_b(SSM_HALO, xbc_dim), seq(n_heads), seq(d_inner), per_b(d_inner, D_STATE),
            _const_spec((SSM_CONV_K, xbc_dim), layer), _const_spec((1, xbc_dim), layer),
            _const_spec((1, n_heads), layer), _const_spec((1, d_inner), layer), _const_spec((1, d_inner), layer),
        ],
        out_specs=(seq(d_inner), per_b(d_inner, D_STATE)),
        out_shape=(jax.ShapeDtypeStruct((b, t, d_inner), BF16),
                   jax.ShapeDtypeStruct((b, d_inner, D_STATE), F32)),
        scratch_shapes=[
            pltpu.VMEM((blk + SSM_HALO, xbc_dim), F32),
            pltpu.VMEM((blk, d_inner), F32),
            pltpu.VMEM((blk, gw), BF16),
            pltpu.VMEM((blk, gw), BF16),
            pltpu.VMEM((blk, d_inner), F32),
            pltpu.VMEM((blk, d_inner), F32),
            pltpu.VMEM((blk, d_inner), F32),
        ],
        compiler_params=_cparams(2),
        name="ssd",
    )(xbc, state_pad, dt, z, h0, wts["ssm_conv_w"], wts["ssm_conv_b"], wts["a_log"], wts["d_skip_c"],
      wts["g_ssm"])


def _merge_kernel(x_ref, c_ref, o_ref, y_ref, gate_ref, wmix_ref, wout_ref, g_ref, out_ref):
    d_conv = c_ref.shape[1]
    attn_w = o_ref.shape[1]
    d_model = x_ref.shape[1]
    branch = (
        _dot(c_ref[...], wmix_ref[0:d_conv, :]),
        _dot(o_ref[...], wmix_ref[d_conv:d_conv + attn_w, :]),
        _dot(y_ref[...], wmix_ref[d_conv + attn_w:, :]),
    )
    merged = sum(gate_ref[:, i * d_model:(i + 1) * d_model].astype(F32) * branch[i] for i in range(3))
    mix = _dot(merged.astype(BF16), wout_ref[...])
    out_ref[...] = x_ref[...] + _rms(mix, g_ref[...])


def _merge(xf, c, o, y, gates, layer, wts, tm):
    m_rows, d_model = xf.shape
    row = lambda w: pl.BlockSpec((tm, w), lambda i: (i, 0))
    mix_w = wts["wmix"].shape[1]
    return pl.pallas_call(
        _merge_kernel,
        grid=(m_rows // tm,),
        in_specs=[row(d_model), row(c.shape[1]), row(o.shape[1]), row(y.shape[1]), row(gates.shape[1]),
                  _const_spec((mix_w, d_model), layer), _const_spec((d_model, d_model), layer),
                  _const_spec((1, d_model), layer)],
        out_specs=row(d_model),
        out_shape=jax.ShapeDtypeStruct((m_rows, d_model), F32),
        compiler_params=_cparams(1),
        name="merge",
    )(xf, c, o, y, gates, wts["wmix"], wts["wout"], wts["g_post_mix"])


def _ffn_kernel(x_ref, gpre_ref, wgu_ref, wd_ref, gpost_ref, out_ref, act_ref, *, n_chunks):
    d_ff = wd_ref.shape[0]
    x = x_ref[...]
    h = _rms(x, gpre_ref[...]).astype(BF16)
    cw = d_ff // n_chunks
    for c in range(n_chunks):
        gate = _dot(h, wgu_ref[:, c * cw:(c + 1) * cw])
        up = _dot(h, wgu_ref[:, d_ff + c * cw:d_ff + (c + 1) * cw])
        act_ref[:, c * cw:(c + 1) * cw] = (_silu(gate) * up).astype(BF16)
    f = _dot(act_ref[...], wd_ref[...])
    out_ref[...] = x + _rms(f, gpost_ref[...])


def _ffn(xf, layer, wts, tm):
    m_rows, d_model = xf.shape
    d_ff = wts["wdown"].shape[1]
    n_chunks = 2 if (d_ff // 2) % LANES == 0 else 1
    row = lambda w: pl.BlockSpec((tm, w), lambda i: (i, 0))
    return pl.pallas_call(
        functools.partial(_ffn_kernel, n_chunks=n_chunks),
        grid=(m_rows // tm,),
        in_specs=[row(d_model), _const_spec((1, d_model), layer), _const_spec((d_model, 2 * d_ff), layer),
                  _const_spec((d_ff, d_model), layer), _const_spec((1, d_model), layer)],
        out_specs=row(d_model),
        out_shape=jax.ShapeDtypeStruct((m_rows, d_model), F32),
        scratch_shapes=[pltpu.VMEM((tm, d_ff), BF16)],
        compiler_params=_cparams(1),
        name="ffn",
    )(xf, wts["g_pre_ffn"], wts["wgu"], wts["wdown"], wts["g_post_ffn"])


def _rotate_half_cols(w):
    half = w.shape[-1] // 2
    return jnp.concatenate([-w[..., half:], w[..., :half]], axis=-1)


def _prep_weights(p, dims):
    d_conv, q_lora, kv_lora, d_inner, xbc_dim, n_ssm_heads, d_model = dims
    depth = p["w_in"].shape[0]
    sizes = (2 * d_conv, q_lora, kv_lora, QK_ROPE, d_inner, xbc_dim, n_ssm_heads, 3 * d_model)
    offs = [0]
    for s in sizes:
        offs.append(offs[-1] + s)
    w_glu, w_q, w_kv, w_kr, w_z, w_xbc, w_dt, w_gate = (p["w_in"][..., offs[i]:offs[i + 1]] for i in range(8))
    misc_pad = LANES - (2 * QK_ROPE + n_ssm_heads)
    w_misc = jnp.concatenate([w_kr, _rotate_half_cols(w_kr), w_dt,
                              jnp.zeros((depth, d_model, misc_pad), F32)], axis=-1)
    w1 = jnp.concatenate([w_glu, w_q, w_kv, w_z, w_xbc, w_gate, w_misc], axis=-1).astype(BF16)

    qk = QK_NOPE + QK_ROPE
    wuq = p["w_uq"].reshape(depth, q_lora, N_HEADS, qk)
    rope_cols = wuq[..., QK_NOPE:]
    wuq = jnp.concatenate([wuq[..., :QK_NOPE], rope_cols, _rotate_half_cols(rope_cols)], axis=-1)
    assert wuq.shape[-1] == HEAD_SLAB
    wuq = wuq.reshape(depth, q_lora, N_HEADS * HEAD_SLAB).astype(BF16)

    wukv = p["w_ukv"].reshape(depth, kv_lora, N_HEADS, QK_NOPE + V_DIM)
    wk = jnp.concatenate([wukv[..., :QK_NOPE], jnp.zeros((depth, kv_lora, N_HEADS, HEAD_SLAB - QK_NOPE), F32)],
                         axis=-1).reshape(depth, kv_lora, N_HEADS * HEAD_SLAB).astype(BF16)
    wv = wukv[..., QK_NOPE:].reshape(depth, kv_lora, N_HEADS * V_DIM).astype(BF16)
    lane_in_slab = jnp.arange(N_HEADS * HEAD_SLAB) % HEAD_SLAB
    pk = (lane_in_slab[None, :] == (QK_NOPE + jnp.arange(QK_ROPE))[:, None]).astype(BF16)

    vec = lambda name: p[name][:, None, :]
    return {
        "w1": w1, "wuq": wuq, "wk": wk, "wv": wv, "pk": pk,
        "wmix": p["w_mix_out"].astype(BF16), "wout": p["w_out"].astype(BF16),
        "wgu": p["w_gate_up"].astype(BF16), "wdown": p["w_down"].astype(BF16),
        "g_pre_mix": vec("g_pre_mix"), "g_post_mix": vec("g_post_mix"),
        "g_pre_ffn": vec("g_pre_ffn"), "g_post_ffn": vec("g_post_ffn"),
        "g_q": vec("g_q"), "g_kv": vec("g_kv"), "dt_bias": vec("dt_bias"), "a_log": vec("a_log"),
        "conv_w": p["conv_w"], "conv_b": vec("conv_b"), "conv_ln_g": vec("conv_ln_g"),
        "conv_ln_b": vec("conv_ln_b"),
        "ssm_conv_w": p["ssm_conv_w"], "ssm_conv_b": vec("ssm_conv_b"),
        "d_skip_c": jnp.repeat(p["d_skip"], SSM_HEAD_DIM, axis=-1)[:, None, :],
        "g_ssm": vec("g_ssm"),
    }


def _rope_tables(n_past, t, rows):
    half = QK_ROPE // 2
    inv_freq = ROPE_BASE ** (-jnp.arange(half, dtype=F32) / half)
    ang = (n_past + jnp.arange(t)).astype(F32)[:, None] * inv_freq[None, :]
    cos, sin = jnp.cos(ang), jnp.sin(ang)
    scale = (QK_NOPE + QK_ROPE) ** -0.5
    ones = jnp.ones((t, QK_NOPE), F32)
    zq = jnp.zeros((t, QK_NOPE), F32)
    zr = jnp.zeros((t, QK_ROPE), F32)
    tabs = {
        "cq": jnp.concatenate([ones, cos, cos, zr], axis=-1) * scale,
        "sq": jnp.concatenate([zq, sin, sin, zr], axis=-1) * scale,
        "ck": jnp.concatenate([cos, cos], axis=-1),
        "sk": jnp.concatenate([sin, sin], axis=-1),
    }
    reps = max(rows // t, 1)
    return {k: jnp.tile(v, (reps, 1)) for k, v in tabs.items()}


def _layer(x, layer, wts, tabs, dims, hist, cfg):
    b, t, d_model = x.shape
    d_conv, q_lora, kv_lora, d_inner, xbc_dim, n_ssm_heads, _ = dims
    m_rows = b * t
    tm = cfg["tm"]
    xf = x.reshape(m_rows, d_model)
    a, q, lat, kr, z, xbc, dt, gates = _inproj(xf, layer, wts, tabs, dims, tm)

    q3 = q.reshape(b, t, -1)
    k_new, v_new = _kvproj(lat, kr, layer, wts, m_rows)
    k_new = k_new.reshape(b, t, -1)
    v_new = v_new.reshape(b, t, -1)
    if hist is None:
        o = _attn_prompt(q3, k_new, v_new, cfg["tq"])
        conv_state = jnp.zeros((b, CONV_HALO, d_conv), F32)
        ssm_conv_state = jnp.zeros((b, SSM_HALO, xbc_dim), F32)
        h0 = jnp.zeros((b, d_inner, D_STATE), F32)
    else:
        n_past = hist["lat"].shape[2]
        depth = hist["lat"].shape[0]
        rows_past = b * n_past
        k_past, v_past = _kvproj(hist["lat"].reshape(depth * rows_past, kv_lora),
                                 hist["kr"].reshape(depth * rows_past, QK_ROPE),
                                 layer, wts, rows_past, row_offset=layer * rows_past)
        o = _attn_hist(q3, k_past.reshape(b, n_past, -1), v_past.reshape(b, n_past, -1), k_new, v_new, n_past)
        conv_state = jnp.pad(hist["conv"][layer], ((0, 0), (CONV_HALO - (CONV_K - 1), 0), (0, 0)))
        ssm_conv_state = jnp.pad(hist["ssm_conv"][layer], ((0, 0), (SSM_HALO - (SSM_CONV_K - 1), 0), (0, 0)))
        h0 = hist["ssm"][layer].reshape(b, d_inner, D_STATE)

    a3 = a.reshape(b, t, d_conv)
    c = _conv(a3, conv_state, layer, wts, cfg["tc"])
    xbc3 = xbc.reshape(b, t, xbc_dim)
    y, h_new = _ssd(xbc3, ssm_conv_state, dt.reshape(b, t, n_ssm_heads), z.reshape(b, t, d_inner), h0,
                    layer, wts, cfg["blk"])

    x1 = _merge(xf, c.reshape(m_rows, d_conv), o.reshape(m_rows, -1), y.reshape(m_rows, d_inner), gates,
                layer, wts, tm)
    x2 = _ffn(x1, layer, wts, tm)

    xbc_f32 = xbc3.astype(F32)
    if hist is None:
        a_hist, xbc_hist = a3, xbc_f32
    else:
        a_hist = jnp.concatenate([hist["conv"][layer], a3], axis=1)
        xbc_hist = jnp.concatenate([hist["ssm_conv"][layer], xbc_f32], axis=1)
    new_state = (
        lat.reshape(b, t, kv_lora),
        kr.reshape(b, t, QK_ROPE),
        a_hist[:, -(CONV_K - 1):],
        xbc_hist[:, -(SSM_CONV_K - 1):],
        h_new.reshape(b, n_ssm_heads, SSM_HEAD_DIM, D_STATE),
    )
    return x2.reshape(b, t, d_model), new_state


def _stream_cfg(b, t):
    m_rows = b * t
    return {
        "tm": _row_tile(m_rows, 512),
        "tq": min(t, 256),
        "tc": min(t, 256),
        "blk": min(t, 256),
    }


def kernel(x_prompt, x_sample, cache_mla_latent, cache_mla_rope, state_conv, state_ssm_conv, state_ssm, g_pre_mix, g_post_mix, g_pre_ffn, g_post_ffn, w_in, conv_w, conv_b, conv_ln_g, conv_ln_b, g_q, w_uq, g_kv, w_ukv, ssm_conv_w, ssm_conv_b, dt_bias, a_log, d_skip, g_ssm, w_mix_out, w_out, w_gate_up, w_down):
    params = dict(g_pre_mix=g_pre_mix, g_post_mix=g_post_mix, g_pre_ffn=g_pre_ffn, g_post_ffn=g_post_ffn,
                  w_in=w_in, conv_w=conv_w, conv_b=conv_b, conv_ln_g=conv_ln_g, conv_ln_b=conv_ln_b,
                  g_q=g_q, w_uq=w_uq, g_kv=g_kv, w_ukv=w_ukv, ssm_conv_w=ssm_conv_w, ssm_conv_b=ssm_conv_b,
                  dt_bias=dt_bias, a_log=a_log, d_skip=d_skip, g_ssm=g_ssm, w_mix_out=w_mix_out, w_out=w_out,
                  w_gate_up=w_gate_up, w_down=w_down)
    depth = w_in.shape[0]
    d_model = x_prompt.shape[-1]
    dims = (conv_w.shape[-1], g_q.shape[-1], g_kv.shape[-1], g_ssm.shape[-1], ssm_conv_w.shape[-1],
            dt_bias.shape[-1], d_model)
    wts = _prep_weights(params, dims)

    bp, tp, _ = x_prompt.shape
    bs, ts, _ = x_sample.shape
    n_past = cache_mla_latent.shape[2]
    cfg_p = _stream_cfg(bp, tp)
    cfg_s = _stream_cfg(bs, ts)
    tabs_p = _rope_tables(0, tp, cfg_p["tm"])
    tabs_s = _rope_tables(n_past, ts, cfg_s["tm"])
    hist = dict(lat=cache_mla_latent, kr=cache_mla_rope, conv=state_conv, ssm_conv=state_ssm_conv, ssm=state_ssm)

    yp, ys = x_prompt, x_sample
    p_states, s_states = [], []
    for layer in range(depth):
        yp, st = _layer(yp, layer, wts, tabs_p, dims, None, cfg_p)
        p_states.append(st)
        ys, st = _layer(ys, layer, wts, tabs_s, dims, hist, cfg_s)
        s_states.append(st)
    stack = lambda states, i: jnp.stack([s[i] for s in states])
    return (yp, ys) + tuple(stack(p_states, i) for i in range(5)) + tuple(stack(s_states, i) for i in range(5))
```

```python
import functools

import jax
import jax.numpy as jnp
from jax import lax
from jax.experimental import pallas as pl
from jax.experimental.pallas import tpu as pltpu

F32 = jnp.float32
BF16 = jnp.bfloat16

CHUNK = 64
EPS = 1e-6
ROPE_BASE = 10000.0
N_HEADS = 8
QK_NOPE = 64
QK_ROPE = 32
V_DIM = 64
SSM_HEAD_DIM = 64
SSM_GROUPS = 4
D_STATE = 128
CONV_K = 31
SSM_CONV_K = 4

LANES = 128
SUBLANES = 8
HEAD_SLAB = LANES
CONV_HALO = 32
SSM_HALO = SUBLANES
ATTN_LOOKAHEAD = 4
VMEM_LIMIT_BYTES = 56 * 1024 * 1024


def _cparams(n_grid):
    return pltpu.CompilerParams(dimension_semantics=("arbitrary",) * n_grid,
                                vmem_limit_bytes=VMEM_LIMIT_BYTES)


def _row_tile(rows, cap):
    best = None
    for t in range(16, min(rows, cap) + 1, 16):
        if rows % t == 0:
            best = t
    assert best is not None, rows
    return best


def _const_spec(shape, layer=None):
    nd = len(shape)
    if layer is None:
        return pl.BlockSpec(shape, lambda *_: (0,) * nd, pipeline_mode=pl.Buffered(1))
    return pl.BlockSpec((None,) + tuple(shape), lambda *_: (layer,) + (0,) * nd,
                        pipeline_mode=pl.Buffered(1))


def _rms(x, g):
    return x * lax.rsqrt(jnp.mean(x * x, axis=-1, keepdims=True) + EPS) * g


def _silu(x):
    return x * jax.nn.sigmoid(x)


def _dot(a, b):
    return jnp.dot(a, b, preferred_element_type=F32)


def _dot_nt(a, b):
    return lax.dot_general(a, b, (((1,), (1,)), ((), ())), preferred_element_type=F32)


def _dot_tn(a, b):
    return lax.dot_general(a, b, (((0,), (0,)), ((), ())), preferred_element_type=F32)


def _split3(x):
    hi = x.astype(BF16)
    r = x - hi.astype(F32)
    mid = r.astype(BF16)
    lo = (r - mid.astype(F32)).astype(BF16)
    return hi, mid, lo


def _inproj_kernel(x_ref, g_ref, w_ref, gq_ref, wuq_ref, gkv_ref, dtb_ref, cq_ref, sq_ref, ck_ref, sk_ref,
                   a_ref, q_ref, lat_ref, kr_ref, z_ref, xbc_ref, dt_ref, gate_ref, *, dims):
    d_conv, q_lora, kv_lora, d_inner, xbc_dim, n_ssm_heads, d_model = dims
    h = _rms(x_ref[...], g_ref[...]).astype(BF16)
    col = [0]

    def proj(width):
        c0 = col[0]
        col[0] = c0 + width
        return _dot(h, w_ref[:, c0:c0 + width])

    u_val = proj(d_conv)
    u_gate = proj(d_conv)
    a_ref[...] = u_val * jax.nn.sigmoid(u_gate)

    hq = _rms(proj(q_lora), gq_ref[...]).astype(BF16)
    qs = _dot(hq, wuq_ref[...])
    cq = cq_ref[...]
    sq = sq_ref[...]
    for hd in range(N_HEADS):
        slab = qs[:, hd * HEAD_SLAB:(hd + 1) * HEAD_SLAB]
        rot = pltpu.roll(slab, HEAD_SLAB - QK_ROPE, axis=1)
        q_ref[:, hd * HEAD_SLAB:(hd + 1) * HEAD_SLAB] = (slab * cq + rot * sq).astype(BF16)

    lat_ref[...] = _rms(proj(kv_lora), gkv_ref[...])

    z_ref[...] = _silu(proj(d_inner)).astype(BF16)
    half = xbc_dim // 2
    xbc_ref[:, :half] = proj(half).astype(BF16)
    xbc_ref[:, half:] = proj(half).astype(BF16)

    for i in range(3):
        gate_ref[:, i * d_model:(i + 1) * d_model] = jax.nn.sigmoid(proj(d_model)).astype(BF16)

    m = proj(LANES)
    kr_ref[...] = m[:, :QK_ROPE] * ck_ref[...] + m[:, QK_ROPE:2 * QK_ROPE] * sk_ref[...]
    dt_raw = m[:, 2 * QK_ROPE:2 * QK_ROPE + n_ssm_heads] + dtb_ref[...]
    dt_ref[...] = jnp.maximum(dt_raw, 0.0) + jnp.log1p(jnp.exp(-jnp.abs(dt_raw)))


def _inproj(xf, layer, wts, tabs, dims, tm):
    m_rows, d_model = xf.shape
    d_conv, q_lora, kv_lora, d_inner, xbc_dim, n_ssm_heads, _ = dims
    n_in = wts["w1"].shape[-1]
    n_tab = tabs["cq"].shape[0] // tm
    row = lambda w: pl.BlockSpec((tm, w), lambda i: (i, 0))
    tab = lambda w: pl.BlockSpec((tm, w), lambda i: (i % n_tab, 0))
    out_shapes = (
        jax.ShapeDtypeStruct((m_rows, d_conv), F32),
        jax.ShapeDtypeStruct((m_rows, N_HEADS * HEAD_SLAB), BF16),
        jax.ShapeDtypeStruct((m_rows, kv_lora), F32),
        jax.ShapeDtypeStruct((m_rows, QK_ROPE), F32),
        jax.ShapeDtypeStruct((m_rows, d_inner), BF16),
        jax.ShapeDtypeStruct((m_rows, xbc_dim), BF16),
        jax.ShapeDtypeStruct((m_rows, n_ssm_heads), F32),
        jax.ShapeDtypeStruct((m_rows, 3 * d_model), BF16),
    )
    return pl.pallas_call(
        functools.partial(_inproj_kernel, dims=dims),
        grid=(m_rows // tm,),
        in_specs=[
            row(d_model),
            _const_spec((1, d_model), layer),
            _const_spec((d_model, n_in), layer),
            _const_spec((1, q_lora), layer),
            _const_spec((q_lora, N_HEADS * HEAD_SLAB), layer),
            _const_spec((1, kv_lora), layer),
            _const_spec((1, n_ssm_heads), layer),
            tab(HEAD_SLAB), tab(HEAD_SLAB), tab(QK_ROPE), tab(QK_ROPE),
        ],
        out_specs=tuple(row(s.shape[1]) for s in out_shapes),
        out_shape=out_shapes,
        compiler_params=_cparams(1),
        name="inproj",
    )(xf, wts["g_pre_mix"], wts["w1"], wts["g_q"], wts["wuq"], wts["g_kv"], wts["dt_bias"],
      tabs["cq"], tabs["sq"], tabs["ck"], tabs["sk"])


def _kvproj_kernel(lat_ref, kr_ref, wk_ref, pk_ref, wv_ref, k_ref, v_ref, *, vt_block):
    lat = lat_ref[...].astype(BF16)
    kr = kr_ref[...].astype(BF16)
    k_ref[...] = (_dot(lat, wk_ref[...]) + _dot(kr, pk_ref[...])).astype(BF16)
    if vt_block is None:
        v_ref[...] = _dot(lat, wv_ref[...]).astype(BF16)
    else:
        for n in range(lat.shape[0] // vt_block):
            v_ref[n] = _dot_nt(wv_ref[...], lat[n * vt_block:(n + 1) * vt_block]).astype(BF16)


def _kvproj(lat, kr, layer, wts, rows, row_offset=0, vt_block=None):
    kv_lora = lat.shape[1]
    tr = _row_tile(rows, 1024)
    assert row_offset % tr == 0
    off = row_offset // tr
    k_w = N_HEADS * HEAD_SLAB
    v_w = N_HEADS * V_DIM
    if vt_block is None:
        wv, wv_spec = wts["wv"], _const_spec((kv_lora, v_w), layer)
        v_spec = pl.BlockSpec((tr, v_w), lambda i: (i, 0))
        v_shape = jax.ShapeDtypeStruct((rows, v_w), BF16)
    else:
        assert tr % vt_block == 0
        wv, wv_spec = wts["wvt"], _const_spec((v_w, kv_lora), layer)
        v_spec = pl.BlockSpec((tr // vt_block, v_w, vt_block), lambda i: (i, 0, 0))
        v_shape = jax.ShapeDtypeStruct((rows // vt_block, v_w, vt_block), BF16)
    return pl.pallas_call(
        functools.partial(_kvproj_kernel, vt_block=vt_block),
        grid=(rows // tr,),
        in_specs=[
            pl.BlockSpec((tr, kv_lora), lambda i: (off + i, 0)),
            pl.BlockSpec((tr, QK_ROPE), lambda i: (off + i, 0)),
            _const_spec((kv_lora, k_w), layer),
            _const_spec((QK_ROPE, k_w)),
            wv_spec,
        ],
        out_specs=(pl.BlockSpec((tr, k_w), lambda i: (i, 0)), v_spec),
        out_shape=(jax.ShapeDtypeStruct((rows, k_w), BF16), v_shape),
        compiler_params=_cparams(1),
        name="kvproj",
    )(lat, kr, wts["wk"], wts["pk"], wv)


def _attn_kernel(q_ref, k_ref, vt_ref, o_ref, m_ref, l_ref, acc_ref, *, tq):
    i = pl.program_id(1)
    k_chunk = lax.broadcasted_iota(jnp.int32, (tq, tq), 0) // CHUNK
    q_chunk = lax.broadcasted_iota(jnp.int32, (tq, tq), 1) // CHUNK
    diag_visible = k_chunk <= q_chunk
    m_ref[...] = jnp.full(m_ref.shape, -jnp.inf, F32)
    l_ref[...] = jnp.zeros(l_ref.shape, F32)
    acc_ref[...] = jnp.zeros(acc_ref.shape, F32)

    def block(kb, masked):
        start = pl.multiple_of(kb * tq, tq)

        def scores(hd):
            hs = slice(hd * HEAD_SLAB, (hd + 1) * HEAD_SLAB)
            return _dot_nt(k_ref[0, pl.ds(start, tq), hs], q_ref[0, :, hs])

        pending = [scores(hd) for hd in range(ATTN_LOOKAHEAD)]
        for hd in range(N_HEADS):
            vs = slice(hd * V_DIM, (hd + 1) * V_DIM)
            if hd + ATTN_LOOKAHEAD < N_HEADS:
                pending.append(scores(hd + ATTN_LOOKAHEAD))
            st = pending.pop(0)
            if masked:
                st = jnp.where(diag_visible, st, -jnp.inf)
            m_old = m_ref[hd:hd + 1, :]
            m_new = jnp.maximum(m_old, jnp.max(st, axis=0, keepdims=True))
            alpha = jnp.exp(m_old - m_new)
            p = jnp.exp(st - m_new)
            l_ref[hd:hd + 1, :] = alpha * l_ref[hd:hd + 1, :] + jnp.sum(p, axis=0, keepdims=True)
            m_ref[hd:hd + 1, :] = m_new
            acc_ref[vs, :] = alpha * acc_ref[vs, :] + _dot(vt_ref[kb, vs, :], p.astype(BF16))

    def loop_body(kb, carry):
        block(kb, False)
        return carry

    lax.fori_loop(0, i, loop_body, 0)
    block(i, True)
    for hd in range(N_HEADS):
        vs = slice(hd * V_DIM, (hd + 1) * V_DIM)
        acc_ref[vs, :] = acc_ref[vs, :] / l_ref[hd:hd + 1, :]
    o_ref[0] = acc_ref[...].T.astype(BF16)


def _attn_prompt(q, k, vt, tq):
    b, t, qw = q.shape
    vw = vt.shape[1]
    nkb = t // tq
    return pl.pallas_call(
        functools.partial(_attn_kernel, tq=tq),
        grid=(b, nkb),
        in_specs=[
            pl.BlockSpec((1, tq, qw), lambda bi, i: (bi, i, 0)),
            pl.BlockSpec((1, t, qw), lambda bi, i: (bi, 0, 0)),
            pl.BlockSpec((nkb, vw, tq), lambda bi, i: (bi, 0, 0)),
        ],
        out_specs=pl.BlockSpec((1, tq, vw), lambda bi, i: (bi, i, 0)),
        out_shape=jax.ShapeDtypeStruct((b, t, vw), BF16),
        scratch_shapes=[pltpu.VMEM((N_HEADS, tq), F32), pltpu.VMEM((N_HEADS, tq), F32),
                        pltpu.VMEM((vw, tq), F32)],
        compiler_params=_cparams(2),
        name="attn_prompt",
    )(q, k, vt)


def _attn_hist_kernel(q_ref, kp_ref, vp_ref, kn_ref, vn_ref, o_ref, *, n_past):
    tq = q_ref.shape[1]
    tp = kp_ref.shape[1]
    lane = lax.broadcasted_iota(jnp.int32, (tq, LANES), 1)
    q_chunk_p = (n_past + lax.broadcasted_iota(jnp.int32, (tq, tp), 0)) // CHUNK
    vis_p = (lax.broadcasted_iota(jnp.int32, (tq, tp), 1) // CHUNK) <= q_chunk_p
    q_chunk_n = (n_past + lax.broadcasted_iota(jnp.int32, (tq, tq), 0)) // CHUNK
    vis_n = ((n_past + lax.broadcasted_iota(jnp.int32, (tq, tq), 1)) // CHUNK) <= q_chunk_n
    pair = LANES // V_DIM
    for hp in range(N_HEADS // pair):
        outs = []
        for jj in range(pair):
            hd = pair * hp + jj
            hs = slice(hd * HEAD_SLAB, (hd + 1) * HEAD_SLAB)
            qh = q_ref[0, :, hs]
            s_p = jnp.where(vis_p, _dot_nt(qh, kp_ref[0, :, hs]), -jnp.inf)
            s_n = jnp.where(vis_n, _dot_nt(qh, kn_ref[0, :, hs]), -jnp.inf)
            m = jnp.maximum(jnp.max(s_p, axis=-1, keepdims=True), jnp.max(s_n, axis=-1, keepdims=True))
            p_p = jnp.exp(s_p - m)
            p_n = jnp.exp(s_n - m)
            l = jnp.sum(p_p, axis=-1, keepdims=True) + jnp.sum(p_n, axis=-1, keepdims=True)
            vs = slice(hp * LANES, (hp + 1) * LANES)
            acc = _dot(p_p.astype(BF16), vp_ref[0, :, vs]) + _dot(p_n.astype(BF16), vn_ref[0, :, vs])
            outs.append(acc / l)
        o_ref[0, :, hp * LANES:(hp + 1) * LANES] = jnp.where(lane < V_DIM, outs[0], outs[1]).astype(BF16)


def _attn_hist(q, kp, vp, kn, vn, n_past):
    b, t, qw = q.shape
    tp = kp.shape[1]
    vw = vp.shape[-1]
    blk = lambda rows, w: pl.BlockSpec((1, rows, w), lambda bi: (bi, 0, 0))
    return pl.pallas_call(
        functools.partial(_attn_hist_kernel, n_past=n_past),
        grid=(b,),
        in_specs=[blk(t, qw), blk(tp, qw), blk(tp, vw), blk(t, qw), blk(t, vw)],
        out_specs=blk(t, vw),
        out_shape=jax.ShapeDtypeStruct((b, t, vw), BF16),
        compiler_params=_cparams(1),
        name="attn_hist",
    )(q, kp, vp, kn, vn)


def _conv_kernel(a_ref, st_ref, w_ref, b_ref, y_ref, pad_ref, odd_ref):
    j = pl.program_id(1)
    n_rows = a_ref.shape[1]
    halo = CONV_HALO // 2

    @pl.when(j == 0)
    def _():
        pad_ref[0:halo] = st_ref[0]

    pad_ref[halo:halo + n_rows] = a_ref[0]
    total = halo + n_rows
    rolled = pltpu.roll(pad_ref[...], SUBLANES // 2, axis=1)
    sub = lax.broadcasted_iota(jnp.int32, (total - 1, SUBLANES, LANES), 1)
    odd_ref[0:total - 1] = jnp.where(sub < SUBLANES // 2, rolled[0:total - 1], rolled[1:total])
    first = CONV_HALO - (CONV_K - 1)
    rb = min(n_rows, 16)
    for r0 in range(0, n_rows, rb):
        acc = jnp.zeros((rb, SUBLANES, LANES), F32) + b_ref[...]
        for k in range(CONV_K):
            d = first + k
            src = pad_ref if d % 2 == 0 else odd_ref
            acc = acc + src[r0 + d // 2:r0 + d // 2 + rb] * w_ref[k]
        y_ref[0, r0:r0 + rb] = acc
    pad_ref[0:halo] = pad_ref[n_rows:n_rows + halo]


def _pack_pairs(x):
    *lead, t, c = x.shape
    assert c == (SUBLANES // 2) * LANES and t % 2 == 0
    return x.reshape(*lead, t // 2, SUBLANES, LANES)


def _conv(a, state_pad, layer, wts, tc):
    b, t, d_conv = a.shape
    n_rows = tc // 2
    blk = lambda r: pl.BlockSpec((1, r, SUBLANES, LANES), lambda bi, j: (bi, j, 0, 0))
    y = pl.pallas_call(
        _conv_kernel,
        grid=(b, t // tc),
        in_specs=[
            blk(n_rows),
            pl.BlockSpec((1, CONV_HALO // 2, SUBLANES, LANES), lambda bi, j: (bi, 0, 0, 0)),
            _const_spec((CONV_K, SUBLANES, LANES), layer),
            _const_spec((SUBLANES, LANES), layer),
        ],
        out_specs=blk(n_rows),
        out_shape=jax.ShapeDtypeStruct((b, t // 2, SUBLANES, LANES), F32),
        scratch_shapes=[pltpu.VMEM((n_rows + CONV_HALO // 2, SUBLANES, LANES), F32),
                        pltpu.VMEM((n_rows + CONV_HALO // 2, SUBLANES, LANES), F32)],
        compiler_params=_cparams(2),
        name="conv",
    )(_pack_pairs(a), _pack_pairs(state_pad), wts["conv_w_p"], wts["conv_b_p"])
    return y.reshape(b, t, d_conv)


def _ssd_kernel(xbc_ref, st_ref, dt_ref, z_ref, h0_ref, cw_ref, cb_ref, alog_ref, dsk_ref, gs_ref,
                y_ref, hT_ref, pad_ref, xs_ref, bm_ref, cm_ref, dtf_ref, csf_ref, yacc_ref):
    j = pl.program_id(1)
    blk = xbc_ref.shape[1]
    xbc_dim = xbc_ref.shape[2]
    d_inner = z_ref.shape[2]
    n_heads = dt_ref.shape[2]
    gw = SSM_GROUPS * D_STATE
    heads_per_group = n_heads // SSM_GROUPS
    group_w = heads_per_group * SSM_HEAD_DIM

    @pl.when(j == 0)
    def _():
        pad_ref[0:SSM_HALO, :] = st_ref[0]
        hT_ref[0] = h0_ref[0]

    pad_ref[SSM_HALO:SSM_HALO + blk, :] = xbc_ref[0].astype(F32)
    first = SSM_HALO - (SSM_CONV_K - 1)
    rc = min(blk, 64)
    cc = 2 * LANES
    for c0 in range(0, xbc_dim, cc):
        for r0 in range(0, blk, rc):
            acc = jnp.zeros((rc, cc), F32) + cb_ref[:, c0:c0 + cc]
            for k in range(SSM_CONV_K):
                lo = r0 + first + k
                acc = acc + pad_ref[lo:lo + rc, c0:c0 + cc] * cw_ref[k:k + 1, c0:c0 + cc]
            acc = _silu(acc)
            if c0 < d_inner:
                xs_ref[r0:r0 + rc, c0:c0 + cc] = acc
            elif c0 < d_inner + gw:
                bm_ref[r0:r0 + rc, c0 - d_inner:c0 - d_inner + cc] = acc.astype(BF16)
            else:
                cm_ref[r0:r0 + rc, c0 - d_inner - gw:c0 - d_inner - gw + cc] = acc.astype(BF16)
    pad_ref[0:SSM_HALO, :] = pad_ref[blk:blk + SSM_HALO, :]

    dt = dt_ref[0]
    da = dt * (-jnp.exp(alog_ref[...]))
    r_i = lax.broadcasted_iota(jnp.int32, (blk, blk), 0)
    c_i = lax.broadcasted_iota(jnp.int32, (blk, blk), 1)
    causal = c_i <= r_i
    tri = causal.astype(BF16)
    tri_u = (r_i <= c_i).astype(BF16)
    da3 = _split3(da)
    cs = sum(_dot(tri, p) for p in da3)
    cs_t = sum(_dot_tn(p, tri_u) for p in da3)
    e_h = lax.broadcasted_iota(jnp.int32, (n_heads, d_inner), 0)
    e_c = lax.broadcasted_iota(jnp.int32, (n_heads, d_inner), 1) // SSM_HEAD_DIM
    expand = (e_h == e_c).astype(BF16)
    dtf_ref[...] = sum(_dot(p, expand) for p in _split3(dt))
    csf_ref[...] = sum(_dot(p, expand) for p in _split3(cs))
    et_r = lax.broadcasted_iota(jnp.int32, (d_inner, n_heads), 0) // SSM_HEAD_DIM
    et_c = lax.broadcasted_iota(jnp.int32, (d_inner, n_heads), 1)
    expand_t = (et_r == et_c).astype(BF16)
    cs_last_col = jnp.broadcast_to(cs_t[:, blk - 1:blk], (n_heads, D_STATE))
    h_decay = jnp.exp(sum(_dot(expand_t, p) for p in _split3(cs_last_col)))

    lane = lax.broadcasted_iota(jnp.int32, (blk, LANES), 1)
    heads_per_slab = LANES // SSM_HEAD_DIM
    for g in range(SSM_GROUPS):
        bg = bm_ref[:, g * D_STATE:(g + 1) * D_STATE]
        cg = cm_ref[:, g * D_STATE:(g + 1) * D_STATE]
        gs = slice(g * group_w, (g + 1) * group_w)
        csf_g = csf_ref[:, gs]
        xdt_g = xs_ref[:, gs] * dtf_ref[:, gs]
        cb_mat = _dot_nt(cg, bg)
        for sl in range(group_w // LANES):
            lo = g * group_w + sl * LANES
            xdt_slab = xdt_g[:, sl * LANES:(sl + 1) * LANES].astype(BF16)
            parts = []
            for hh in range(heads_per_slab):
                hd = lo // SSM_HEAD_DIM + hh
                seg = cs[:, hd:hd + 1] - cs_t[hd:hd + 1, :]
                dec = jnp.exp(jnp.where(causal, seg, -jnp.inf))
                parts.append(_dot((cb_mat * dec).astype(BF16), xdt_slab))
            yacc_ref[:, lo:lo + LANES] = jnp.where(lane < SSM_HEAD_DIM, parts[0], parts[1])
        h_old = hT_ref[0, gs, :]
        y_off = _dot_nt(cg, h_old.astype(BF16)) * jnp.exp(csf_g)
        yacc_ref[:, gs] = yacc_ref[:, gs] + y_off
        to_end = jnp.exp(csf_g[blk - 1:blk, :] - csf_g)
        h_in = _dot_tn((xdt_g * to_end).astype(BF16), bg)
        hT_ref[0, gs, :] = h_old * h_decay[gs, :] + h_in

    norm_w = d_inner // SSM_GROUPS
    for g in range(SSM_GROUPS):
        gs = slice(g * norm_w, (g + 1) * norm_w)
        y = (yacc_ref[:, gs] + xs_ref[:, gs] * dsk_ref[:, gs]) * z_ref[0, :, gs].astype(F32)
        y_ref[0, :, gs] = _rms(y, gs_ref[:, gs]).astype(BF16)


def _ssd(xbc, state_pad, dt, z, h0, layer, wts, blk):
    b, t, xbc_dim = xbc.shape
    d_inner = z.shape[-1]
    n_heads = dt.shape[-1]
    gw = SSM_GROUPS * D_STATE
    seq = lambda w: pl.BlockSpec((1, blk, w), lambda bi, j: (bi, j, 0))
    per_b = lambda r, w: pl.BlockSpec((1, r, w), lambda bi, j: (bi, 0, 0))
    return pl.pallas_call(
        _ssd_kernel,
        grid=(b, t // blk),
        in_specs=[
            seq(xbc_dim), per_b(SSM_HALO, xbc_dim), seq(n_heads), seq(d_inner), per_b(d_inner, D_STATE),
            _const_spec((SSM_CONV_K, xbc_dim), layer), _const_spec((1, xbc_dim), layer),
            _const_spec((1, n_heads), layer), _const_spec((1, d_inner), layer), _const_spec((1, d_inner), layer),
        ],
        out_specs=(seq(d_inner), per_b(d_inner, D_STATE)),
        out_shape=(jax.ShapeDtypeStruct((b, t, d_inner), BF16),
                   jax.ShapeDtypeStruct((b, d_inner, D_STATE), F32)),
        scratch_shapes=[
            pltpu.VMEM((blk + SSM_HALO, xbc_dim), F32),
            pltpu.VMEM((blk, d_inner), F32),
            pltpu.VMEM((blk, gw), BF16),
            pltpu.VMEM((blk, gw), BF16),
            pltpu.VMEM((blk, d_inner), F32),
            pltpu.VMEM((blk, d_inner), F32),
            pltpu.VMEM((blk, d_inner), F32),
        ],
        compiler_params=_cparams(2),
        name="ssd",
    )(xbc, state_pad, dt, z, h0, wts["ssm_conv_w"], wts["ssm_conv_b"], wts["a_log"], wts["d_skip_c"],
      wts["g_ssm"])


def _merge_kernel(x_ref, c_ref, o_ref, y_ref, gate_ref, lg_ref, lb_ref, wmix_ref, wout_ref, g_ref, out_ref):
    d_conv = c_ref.shape[1]
    attn_w = o_ref.shape[1]
    d_model = x_ref.shape[1]
    cv = c_ref[...]
    cc = cv - jnp.mean(cv, axis=-1, keepdims=True)
    cn = cc * lax.rsqrt(jnp.mean(cc * cc, axis=-1, keepdims=True) + EPS) * lg_ref[...] + lb_ref[...]
    branch = (
        _dot(_silu(cn).astype(BF16), wmix_ref[0:d_conv, :]),
        _dot(o_ref[...], wmix_ref[d_conv:d_conv + attn_w, :]),
        _dot(y_ref[...], wmix_ref[d_conv + attn_w:, :]),
    )
    merged = sum(gate_ref[:, i * d_model:(i + 1) * d_model].astype(F32) * branch[i] for i in range(3))
    mix = _dot(merged.astype(BF16), wout_ref[...])
    out_ref[...] = x_ref[...] + _rms(mix, g_ref[...])


def _merge(xf, c, o, y, gates, layer, wts, tm):
    m_rows, d_model = xf.shape
    row = lambda w: pl.BlockSpec((tm, w), lambda i: (i, 0))
    mix_w = wts["wmix"].shape[1]
    return pl.pallas_call(
        _merge_kernel,
        grid=(m_rows // tm,),
        in_specs=[row(d_model), row(c.shape[1]), row(o.shape[1]), row(y.shape[1]), row(gates.shape[1]),
                  _const_spec((1, c.shape[1]), layer), _const_spec((1, c.shape[1]), layer),
                  _const_spec((mix_w, d_model), layer), _const_spec((d_model, d_model), layer),
                  _const_spec((1, d_model), layer)],
        out_specs=row(d_model),
        out_shape=jax.ShapeDtypeStruct((m_rows, d_model), F32),
        compiler_params=_cparams(1),
        name="merge",
    )(xf, c, o, y, gates, wts["conv_ln_g"], wts["conv_ln_b"], wts["wmix"], wts["wout"], wts["g_post_mix"])


def _ffn_kernel(x_ref, gpre_ref, wgu_ref, wd_ref, gpost_ref, out_ref, act_ref, *, n_chunks):
    d_ff = wd_ref.shape[0]
    x = x_ref[...]
    h = _rms(x, gpre_ref[...]).astype(BF16)
    cw = d_ff // n_chunks
    for c in range(n_chunks):
        gate = _dot(h, wgu_ref[:, c * cw:(c + 1) * cw])
        up = _dot(h, wgu_ref[:, d_ff + c * cw:d_ff + (c + 1) * cw])
        act_ref[:, c * cw:(c + 1) * cw] = (_silu(gate) * up).astype(BF16)
    f = _dot(act_ref[...], wd_ref[...])
    out_ref[...] = x + _rms(f, gpost_ref[...])


def _ffn(xf, layer, wts, tm):
    m_rows, d_model = xf.shape
    d_ff = wts["wdown"].shape[1]
    n_chunks = 2 if (d_ff // 2) % LANES == 0 else 1
    row = lambda w: pl.BlockSpec((tm, w), lambda i: (i, 0))
    return pl.pallas_call(
        functools.partial(_ffn_kernel, n_chunks=n_chunks),
        grid=(m_rows // tm,),
        in_specs=[row(d_model), _const_spec((1, d_model), layer), _const_spec((d_model, 2 * d_ff), layer),
                  _const_spec((d_ff, d_model), layer), _const_spec((1, d_model), layer)],
        out_specs=row(d_model),
        out_shape=jax.ShapeDtypeStruct((m_rows, d_model), F32),
        scratch_shapes=[pltpu.VMEM((tm, d_ff), BF16)],
        compiler_params=_cparams(1),
        name="ffn",
    )(xf, wts["g_pre_ffn"], wts["wgu"], wts["wdown"], wts["g_post_ffn"])


def _rotate_half_cols(w):
    half = w.shape[-1] // 2
    return jnp.concatenate([-w[..., half:], w[..., :half]], axis=-1)


def _prep_weights(p, dims):
    d_conv, q_lora, kv_lora, d_inner, xbc_dim, n_ssm_heads, d_model = dims
    depth = p["w_in"].shape[0]
    sizes = (2 * d_conv, q_lora, kv_lora, QK_ROPE, d_inner, xbc_dim, n_ssm_heads, 3 * d_model)
    offs = [0]
    for s in sizes:
        offs.append(offs[-1] + s)
    w_glu, w_q, w_kv, w_kr, w_z, w_xbc, w_dt, w_gate = (p["w_in"][..., offs[i]:offs[i + 1]] for i in range(8))
    misc_pad = LANES - (2 * QK_ROPE + n_ssm_heads)
    w_misc = jnp.concatenate([w_kr, _rotate_half_cols(w_kr), w_dt,
                              jnp.zeros((depth, d_model, misc_pad), F32)], axis=-1)
    w1 = jnp.concatenate([w_glu, w_q, w_kv, w_z, w_xbc, w_gate, w_misc], axis=-1).astype(BF16)

    qk = QK_NOPE + QK_ROPE
    wuq = p["w_uq"].reshape(depth, q_lora, N_HEADS, qk)
    rope_cols = wuq[..., QK_NOPE:]
    wuq = jnp.concatenate([wuq[..., :QK_NOPE], rope_cols, _rotate_half_cols(rope_cols)], axis=-1)
    assert wuq.shape[-1] == HEAD_SLAB
    wuq = wuq.reshape(depth, q_lora, N_HEADS * HEAD_SLAB).astype(BF16)

    wukv = p["w_ukv"].reshape(depth, kv_lora, N_HEADS, QK_NOPE + V_DIM)
    wk = jnp.concatenate([wukv[..., :QK_NOPE], jnp.zeros((depth, kv_lora, N_HEADS, HEAD_SLAB - QK_NOPE), F32)],
                         axis=-1).reshape(depth, kv_lora, N_HEADS * HEAD_SLAB).astype(BF16)
    wv = wukv[..., QK_NOPE:].reshape(depth, kv_lora, N_HEADS * V_DIM).astype(BF16)
    lane_in_slab = jnp.arange(N_HEADS * HEAD_SLAB) % HEAD_SLAB
    pk = (lane_in_slab[None, :] == (QK_NOPE + jnp.arange(QK_ROPE))[:, None]).astype(BF16)

    per_tile = lambda w: jnp.tile(w.reshape(*w.shape[:-1], SUBLANES // 2, LANES),
                                  (1,) * (w.ndim - 1) + (2, 1))

    vec = lambda name: p[name][:, None, :]
    return {
        "w1": w1, "wuq": wuq, "wk": wk, "wv": wv, "wvt": jnp.swapaxes(wv, 1, 2), "pk": pk,
        "conv_w_p": per_tile(p["conv_w"]), "conv_b_p": per_tile(p["conv_b"]),
        "wmix": p["w_mix_out"].astype(BF16), "wout": p["w_out"].astype(BF16),
        "wgu": p["w_gate_up"].astype(BF16), "wdown": p["w_down"].astype(BF16),
        "g_pre_mix": vec("g_pre_mix"), "g_post_mix": vec("g_post_mix"),
        "g_pre_ffn": vec("g_pre_ffn"), "g_post_ffn": vec("g_post_ffn"),
        "g_q": vec("g_q"), "g_kv": vec("g_kv"), "dt_bias": vec("dt_bias"), "a_log": vec("a_log"),
        "conv_w": p["conv_w"], "conv_b": vec("conv_b"), "conv_ln_g": vec("conv_ln_g"),
        "conv_ln_b": vec("conv_ln_b"),
        "ssm_conv_w": p["ssm_conv_w"], "ssm_conv_b": vec("ssm_conv_b"),
        "d_skip_c": jnp.repeat(p["d_skip"], SSM_HEAD_DIM, axis=-1)[:, None, :],
        "g_ssm": vec("g_ssm"),
    }


def _rope_tables(n_past, t, rows):
    half = QK_ROPE // 2
    inv_freq = ROPE_BASE ** (-jnp.arange(half, dtype=F32) / half)
    ang = (n_past + jnp.arange(t)).astype(F32)[:, None] * inv_freq[None, :]
    cos, sin = jnp.cos(ang), jnp.sin(ang)
    scale = (QK_NOPE + QK_ROPE) ** -0.5
    ones = jnp.ones((t, QK_NOPE), F32)
    zq = jnp.zeros((t, QK_NOPE), F32)
    zr = jnp.zeros((t, QK_ROPE), F32)
    tabs = {
        "cq": jnp.concatenate([ones, cos, cos, zr], axis=-1) * scale,
        "sq": jnp.concatenate([zq, sin, sin, zr], axis=-1) * scale,
        "ck": jnp.concatenate([cos, cos], axis=-1),
        "sk": jnp.concatenate([sin, sin], axis=-1),
    }
    reps = max(rows // t, 1)
    return {k: jnp.tile(v, (reps, 1)) for k, v in tabs.items()}


def _layer(x, layer, wts, tabs, dims, hist, cfg):
    b, t, d_model = x.shape
    d_conv, q_lora, kv_lora, d_inner, xbc_dim, n_ssm_heads, _ = dims
    m_rows = b * t
    tm = cfg["tm"]
    xf = x.reshape(m_rows, d_model)
    a, q, lat, kr, z, xbc, dt, gates = _inproj(xf, layer, wts, tabs, dims, tm)

    q3 = q.reshape(b, t, -1)
    if hist is None:
        k_new, vt_new = _kvproj(lat, kr, layer, wts, m_rows, vt_block=cfg["tq"])
        o = _attn_prompt(q3, k_new.reshape(b, t, -1), vt_new, cfg["tq"])
        conv_state = jnp.zeros((b, CONV_HALO, d_conv), F32)
        ssm_conv_state = jnp.zeros((b, SSM_HALO, xbc_dim), F32)
        h0 = jnp.zeros((b, d_inner, D_STATE), F32)
    else:
        n_past = hist["lat"].shape[2]
        depth = hist["lat"].shape[0]
        rows_past = b * n_past
        k_new, v_new = _kvproj(lat, kr, layer, wts, m_rows)
        k_new = k_new.reshape(b, t, -1)
        v_new = v_new.reshape(b, t, -1)
        k_past, v_past = _kvproj(hist["lat"].reshape(depth * rows_past, kv_lora),
                                 hist["kr"].reshape(depth * rows_past, QK_ROPE),
                                 layer, wts, rows_past, row_offset=layer * rows_past)
        o = _attn_hist(q3, k_past.reshape(b, n_past, -1), v_past.reshape(b, n_past, -1), k_new, v_new, n_past)
        conv_state = jnp.pad(hist["conv"][layer], ((0, 0), (CONV_HALO - (CONV_K - 1), 0), (0, 0)))
        ssm_conv_state = jnp.pad(hist["ssm_conv"][layer], ((0, 0), (SSM_HALO - (SSM_CONV_K - 1), 0), (0, 0)))
        h0 = hist["ssm"][layer].reshape(b, d_inner, D_STATE)

    a3 = a.reshape(b, t, d_conv)
    c = _conv(a3, conv_state, layer, wts, cfg["tc"])
    xbc3 = xbc.reshape(b, t, xbc_dim)
    y, h_new = _ssd(xbc3, ssm_conv_state, dt.reshape(b, t, n_ssm_heads), z.reshape(b, t, d_inner), h0,
                    layer, wts, cfg["blk"])

    x1 = _merge(xf, c.reshape(m_rows, d_conv), o.reshape(m_rows, -1), y.reshape(m_rows, d_inner), gates,
                layer, wts, tm)
    x2 = _ffn(x1, layer, wts, tm)

    xbc_f32 = xbc3.astype(F32)
    if hist is None:
        a_hist, xbc_hist = a3, xbc_f32
    else:
        a_hist = jnp.concatenate([hist["conv"][layer], a3], axis=1)
        xbc_hist = jnp.concatenate([hist["ssm_conv"][layer], xbc_f32], axis=1)
    new_state = (
        lat.reshape(b, t, kv_lora),
        kr.reshape(b, t, QK_ROPE),
        a_hist[:, -(CONV_K - 1):],
        xbc_hist[:, -(SSM_CONV_K - 1):],
        h_new.reshape(b, n_ssm_heads, SSM_HEAD_DIM, D_STATE),
    )
    return x2.reshape(b, t, d_model), new_state


def _stream_cfg(b, t):
    m_rows = b * t
    return {
        "tm": _row_tile(m_rows, 512),
        "tq": min(t, 256),
        "tc": min(t, 256),
        "blk": min(t, 256),
    }


def kernel(x_prompt, x_sample, cache_mla_latent, cache_mla_rope, state_conv, state_ssm_conv, state_ssm, g_pre_mix, g_post_mix, g_pre_ffn, g_post_ffn, w_in, conv_w, conv_b, conv_ln_g, conv_ln_b, g_q, w_uq, g_kv, w_ukv, ssm_conv_w, ssm_conv_b, dt_bias, a_log, d_skip, g_ssm, w_mix_out, w_out, w_gate_up, w_down):
    params = dict(g_pre_mix=g_pre_mix, g_post_mix=g_post_mix, g_pre_ffn=g_pre_ffn, g_post_ffn=g_post_ffn,
                  w_in=w_in, conv_w=conv_w, conv_b=conv_b, conv_ln_g=conv_ln_g, conv_ln_b=conv_ln_b,
                  g_q=g_q, w_uq=w_uq, g_kv=g_kv, w_ukv=w_ukv, ssm_conv_w=ssm_conv_w, ssm_conv_b=ssm_conv_b,
                  dt_bias=dt_bias, a_log=a_log, d_skip=d_skip, g_ssm=g_ssm, w_mix_out=w_mix_out, w_out=w_out,
                  w_gate_up=w_gate_up, w_down=w_down)
    depth = w_in.shape[0]
    d_model = x_prompt.shape[-1]
    dims = (conv_w.shape[-1], g_q.shape[-1], g_kv.shape[-1], g_ssm.shape[-1], ssm_conv_w.shape[-1],
            dt_bias.shape[-1], d_model)
    wts = _prep_weights(params, dims)

    bp, tp, _ = x_prompt.shape
    bs, ts, _ = x_sample.shape
    n_past = cache_mla_latent.shape[2]
    cfg_p = _stream_cfg(bp, tp)
    cfg_s = _stream_cfg(bs, ts)
    tabs_p = _rope_tables(0, tp, cfg_p["tm"])
    tabs_s = _rope_tables(n_past, ts, cfg_s["tm"])
    hist = dict(lat=cache_mla_latent, kr=cache_mla_rope, conv=state_conv, ssm_conv=state_ssm_conv, ssm=state_ssm)

    yp, ys = x_prompt, x_sample
    p_states, s_states = [], []
    for layer in range(depth):
        yp, st = _layer(yp, layer, wts, tabs_p, dims, None, cfg_p)
        p_states.append(st)
        ys, st = _layer(ys, layer, wts, tabs_s, dims, hist, cfg_s)
        s_states.append(st)
    stack = lambda states, i: jnp.stack([s[i] for s in states])
    return (yp, ys) + tuple(stack(p_states, i) for i in range(5)) + tuple(stack(s_states, i) for i in range(5))
```
